```python
import jax, jax.numpy as jnp
from jax import lax
import numpy as np

D_MODEL = 2048
BATCH = 4
SEQ = 2048
DEPTH = 2

POOL_WINDOWS = (2, 4, 8, 16)
N_POOL_GROUPS = len(POOL_WINDOWS)
D_POOL = D_MODEL // 2
POOL_GROUP = D_POOL // N_POOL_GROUPS
D_CONV = D_MODEL // 2
N_CONV_HEADS = 8
CONV_HEAD = D_CONV // N_CONV_HEADS
CONV_WIDTH = 3
D_MIX_EVEN = D_POOL + D_CONV
D_IN_EVEN = D_POOL + 3 * D_CONV
CHUNK = 128
D_SGU = D_MODEL
N_SGU_HEADS = 8
SGU_HEAD = D_SGU // N_SGU_HEADS
N_EXPERTS = 16
N_EXPERT_GROUPS = 4
EXPERTS_PER_GROUP = N_EXPERTS // N_EXPERT_GROUPS
TOP_K = 2
D_EXPERT = 512
N_MOD = 6
EPS = 1e-6
N_EVEN = (DEPTH + 1) // 2
N_ODD = DEPTH // 2

kernel_name = "hybrid_pool_conv_sgu_grouped_moe_adaln"


def rmsnorm(x, g):
    xf = x.astype(jnp.float32)
    y = xf * lax.rsqrt(jnp.mean(xf * xf, axis=-1, keepdims=True) + EPS)
    return (y * g.astype(jnp.float32)).astype(x.dtype)


def modulate(x, g, shift, scale):
    return rmsnorm(x, g) * (1 + scale[:, None, :]) + shift[:, None, :]


def pool_mixer(u, w_pool, pool_scale):
    b, s, _ = u.shape
    ug = u.astype(jnp.float32).reshape(b, s, N_POOL_GROUPS, POOL_GROUP)
    cs = jnp.cumsum(ug, axis=1)
    pos = jnp.arange(1, s + 1, dtype=jnp.float32)
    means = []
    for g, w in enumerate(POOL_WINDOWS):
        cg = cs[:, :, g]
        lag = jnp.pad(cg, ((0, 0), (w, 0), (0, 0)))[:, :s]
        means.append((cg - lag) / jnp.minimum(pos, float(w))[None, :, None])
    pooled = (jnp.stack(means, axis=2) - ug).astype(u.dtype)
    y = jnp.einsum('bsgc,gcd->bsgd', pooled, w_pool).reshape(b, s, D_POOL)
    return y * pool_scale


def conv_mixer(h, gate_b, gate_c, conv_w):
    b, s, _ = h.shape
    v = (gate_c * h).reshape(b, s, N_CONV_HEADS, CONV_HEAD)
    vp = jnp.pad(v, ((0, 0), (CONV_WIDTH - 1, 0), (0, 0), (0, 0)))
    z = sum(conv_w[k] * vp[:, k:k + s] for k in range(CONV_WIDTH))
    return gate_b * z.reshape(b, s, D_CONV)


def sgu_mixer(z, sgu_norm, w_spatial, b_spatial):
    b, s, _ = z.shape
    z = jax.nn.gelu(z)
    u, v = jnp.split(z, 2, axis=-1)
    v = rmsnorm(v, sgu_norm)
    n = s // CHUNK
    v = v.reshape(b, n, CHUNK, N_SGU_HEADS, SGU_HEAD)
    causal = jnp.tril(jnp.ones((CHUNK, CHUNK), dtype=w_spatial.dtype))
    w = w_spatial * causal[None]
    mixed = jnp.einsum('hts,bnshd->bnthd', w, v) + jnp.transpose(b_spatial)[None, None, :, :, None]
    return u * mixed.reshape(b, s, D_SGU)


def grouped_moe(h, w_router, router_bias, w_gate, w_up, w_down):
    b, s, d = h.shape
    t = h.reshape(b * s, d)
    aff = jax.nn.sigmoid(jnp.matmul(t, w_router).astype(jnp.float32))
    sel = aff + router_bias.astype(jnp.float32)
    grp = sel.reshape(-1, N_EXPERT_GROUPS, EXPERTS_PER_GROUP)
    group_score = jnp.sum(lax.top_k(grp, TOP_K)[0], axis=-1)
    best = jnp.argmax(group_score, axis=-1)
    in_group = (jnp.arange(N_EXPERTS) // EXPERTS_PER_GROUP)[None, :] == best[:, None]
    _, idx = lax.top_k(jnp.where(in_group, sel, -jnp.inf), TOP_K)
    wts = jnp.take_along_axis(aff, idx, axis=-1)
    wts = wts / jnp.sum(wts, axis=-1, keepdims=True)
    combine = jnp.sum(jax.nn.one_hot(idx, N_EXPERTS, dtype=jnp.float32) * wts[..., None], axis=1)
    g = jnp.einsum('td,edf->tef', t, w_gate)
    u = jnp.einsum('td,edf->tef', t, w_up)
    a = jax.nn.silu(g) * u * combine.astype(t.dtype)[..., None]
    y = jnp.einsum('tef,efd->td', a, w_down)
    return y.reshape(b, s, d)


def setup_inputs(seed: int = 0) -> dict:
    key = jax.random.key(seed)
    ks = jax.random.split(key, 24)
    f32 = jnp.float32
    nrm = lambda k, shape, scale: jax.random.normal(k, shape, f32) * scale
    D = D_MODEL
    return {
        "x": nrm(ks[0], (BATCH, SEQ, D), 1.0),
        "c": nrm(ks[1], (BATCH, D), 1.0),
        "ada_w": nrm(ks[2], (DEPTH, D, N_MOD * D), 0.5 * D ** -0.5),
        "ada_b": nrm(ks[3], (DEPTH, N_MOD * D), 0.02),
        "norm_mix": 1.0 + nrm(ks[4], (DEPTH, D), 0.02),
        "norm_ffn": 1.0 + nrm(ks[5], (DEPTH, D), 0.02),
        "w_in_even": nrm(ks[6], (N_EVEN, D, D_IN_EVEN), D ** -0.5),
        "w_pool": nrm(ks[7], (N_EVEN, N_POOL_GROUPS, POOL_GROUP, POOL_GROUP), POOL_GROUP ** -0.5),
        "pool_scale": 1.0 + nrm(ks[8], (N_EVEN, D_POOL), 0.02),
        "conv_w": nrm(ks[9], (N_EVEN, CONV_WIDTH, N_CONV_HEADS, CONV_HEAD), CONV_WIDTH ** -0.5),
        "w_out_even": nrm(ks[10], (N_EVEN, D_MIX_EVEN, D), D_MIX_EVEN ** -0.5),
        "w_in_odd": nrm(ks[11], (N_ODD, D, 2 * D_SGU), D ** -0.5),
        "sgu_norm": 1.0 + nrm(ks[12], (N_ODD, D_SGU), 0.02),
        "w_spatial": nrm(ks[13], (N_ODD, N_SGU_HEADS, CHUNK, CHUNK), CHUNK ** -0.5),
        "b_spatial": nrm(ks[14], (N_ODD, N_SGU_HEADS, CHUNK), 0.02),
        "w_out_odd": nrm(ks[15], (N_ODD, D_SGU, D), D_SGU ** -0.5),
        "w_router": nrm(ks[16], (D, N_EXPERTS), D ** -0.5),
        "router_bias": nrm(ks[17], (N_EXPERTS,), 0.01),
        "w_gate": nrm(ks[18], (DEPTH, N_EXPERTS, D, D_EXPERT), D ** -0.5),
        "w_up": nrm(ks[19], (DEPTH, N_EXPERTS, D, D_EXPERT), D ** -0.5),
        "w_down": nrm(ks[20], (DEPTH, N_EXPERTS, D_EXPERT, D), D_EXPERT ** -0.5),
        "final_norm": 1.0 + nrm(ks[21], (D,), 0.02),
    }


def reference(x, c, ada_w, ada_b, norm_mix, norm_ffn, w_in_even, w_pool, pool_scale, conv_w,
              w_out_even, w_in_odd, sgu_norm, w_spatial, b_spatial, w_out_odd, w_router,
              router_bias, w_gate, w_up, w_down, final_norm):
    c_act = jax.nn.silu(c)
    for l in range(DEPTH):
        mod = jnp.matmul(c_act, ada_w[l]) + ada_b[l]
        shift_m, scale_m, gate_m, shift_f, scale_f, gate_f = jnp.split(mod, N_MOD, axis=-1)
        h = modulate(x, norm_mix[l], shift_m, scale_m)
        if l % 2 == 0:
            i = l // 2
            z = jnp.matmul(h, w_in_even[i])
            z_pool, z_h, z_b, z_c = jnp.split(
                z, [D_POOL, D_POOL + D_CONV, D_POOL + 2 * D_CONV], axis=-1)
            y_a = pool_mixer(z_pool, w_pool[i], pool_scale[i])
            y_b = conv_mixer(z_h, z_b, z_c, conv_w[i])
            y = jnp.matmul(jnp.concatenate([y_a, y_b], axis=-1), w_out_even[i])
        else:
            i = l // 2
            z = jnp.matmul(h, w_in_odd[i])
            y = jnp.matmul(sgu_mixer(z, sgu_norm[i], w_spatial[i], b_spatial[i]), w_out_odd[i])
        x = x + gate_m[:, None, :] * y
        h = modulate(x, norm_ffn[l], shift_f, scale_f)
        x = x + gate_f[:, None, :] * grouped_moe(h, w_router, router_bias, w_gate[l], w_up[l], w_down[l])
    return rmsnorm(x, final_norm)
```

```python
import functools

import jax
import jax.numpy as jnp
from jax import lax
from jax.experimental import pallas as pl
from jax.experimental.pallas import tpu as pltpu

F32 = jnp.float32
BF16 = jnp.bfloat16

EPS = 1e-6
POOL_WINDOWS = (2, 4, 8, 16)
MAX_WINDOW = max(POOL_WINDOWS)
CONV_WIDTH = 3
CONV_HALO = 8
CHUNK = 128
N_SGU_HEADS = 8
N_EXPERTS = 16
N_EXPERT_GROUPS = 4
EXPERTS_PER_GROUP = N_EXPERTS // N_EXPERT_GROUPS
N_MOD = 6
LANES = 128

ADA_ROWS = 16
ADA_TN = 1024
MIX_TM = 256
ROUTE_TM = 512
EXPERT_TM = 256
COMBINE_TM = 256
VMEM_LIMIT = 56 * 1024 * 1024


def _rmsnorm(x, g):
    return x * lax.rsqrt(jnp.mean(x * x, axis=-1, keepdims=True) + EPS) * g


def _modulate(x, g, shift, scale):
    return _rmsnorm(x, g) * (1.0 + scale) + shift


def _params(*sem):
    return pltpu.CompilerParams(dimension_semantics=sem, vmem_limit_bytes=VMEM_LIMIT)


def _resident(shape):
    return pl.BlockSpec(shape, lambda *_: (0,) * len(shape), pipeline_mode=pl.Buffered(1))


def _ada_kernel(c_ref, w_ref, b_ref, o_ref):
    c_act = jax.nn.silu(c_ref[...]).astype(BF16)
    o_ref[0] = jnp.dot(c_act, w_ref[0].astype(BF16), preferred_element_type=F32) + b_ref[0]


def _ada(c, ada_w, ada_b):
    depth, d, n = ada_w.shape
    batch = c.shape[0]
    c_pad = jnp.zeros((ADA_ROWS, d), F32).at[:batch].set(c)
    out = pl.pallas_call(
        _ada_kernel,
        grid=(depth, n // ADA_TN),
        in_specs=[
            pl.BlockSpec((ADA_ROWS, d), lambda l, j: (0, 0)),
            pl.BlockSpec((1, d, ADA_TN), lambda l, j: (l, 0, j)),
            pl.BlockSpec((1, 1, ADA_TN), lambda l, j: (l, 0, j)),
        ],
        out_specs=pl.BlockSpec((1, ADA_ROWS, ADA_TN), lambda l, j: (l, 0, j)),
        out_shape=jax.ShapeDtypeStruct((depth, ADA_ROWS, n), F32),
        compiler_params=_params("arbitrary", "arbitrary"),
        name="ada",
    )(c_pad, ada_w, ada_b.reshape(depth, 1, n))
    return out[:, :batch]


def _mix_even_kernel(x_ref, shift_ref, scale_ref, gate_ref, g_ref, win_ref, wpool_ref, pscale_ref, cw_ref,
                     wout_ref, o_ref, ext_pool, ext_conv, *, tm, d_pool):
    s = pl.program_id(1)
    group = d_pool // len(POOL_WINDOWS)

    @pl.when(s == 0)
    def _():
        ext_pool[0:MAX_WINDOW, :] = jnp.zeros((MAX_WINDOW, d_pool), F32)
        ext_conv[0:CONV_HALO, :] = jnp.zeros((CONV_HALO, d_pool), F32)

    x = x_ref[0]
    h = _modulate(x, g_ref[...], shift_ref[0], scale_ref[0]).astype(BF16)
    z = jnp.dot(h, win_ref[...], preferred_element_type=F32)
    u = z[:, :d_pool]
    z_h = z[:, d_pool:2 * d_pool]
    z_b = z[:, 2 * d_pool:3 * d_pool]
    z_c = z[:, 3 * d_pool:]
    ext_pool[MAX_WINDOW:MAX_WINDOW + tm, :] = u
    ext_conv[CONV_HALO:CONV_HALO + tm, :] = z_c * z_h

    pos = (s * tm + 1 + lax.broadcasted_iota(jnp.int32, (tm, 1), 0)).astype(F32)
    parts = []
    for gi, w in enumerate(POOL_WINDOWS):
        cols = slice(gi * group, (gi + 1) * group)
        acc = ext_pool[MAX_WINDOW:MAX_WINDOW + tm, cols]
        for j in range(1, w):
            acc = acc + ext_pool[MAX_WINDOW - j:MAX_WINDOW - j + tm, cols]
        pooled = (acc / jnp.minimum(pos, float(w)) - u[:, cols]).astype(BF16)
        parts.append(jnp.dot(pooled, wpool_ref[gi], preferred_element_type=F32) * pscale_ref[:, cols])

    conv = cw_ref[0:1, :] * ext_conv[CONV_HALO - 2:CONV_HALO - 2 + tm, :]
    for k in range(1, CONV_WIDTH):
        lo = CONV_HALO - (CONV_WIDTH - 1) + k
        conv = conv + cw_ref[k:k + 1, :] * ext_conv[lo:lo + tm, :]
    parts.append(z_b * conv)

    mixed = jnp.concatenate(parts, axis=-1).astype(BF16)
    y = jnp.dot(mixed, wout_ref[...], preferred_element_type=F32)
    o_ref[0] = x + gate_ref[0] * y

    ext_pool[0:MAX_WINDOW, :] = ext_pool[tm:tm + MAX_WINDOW, :]
    ext_conv[0:CONV_HALO, :] = ext_conv[tm:tm + CONV_HALO, :]


def _mix_even(x, shift, scale, gate, g, w_in, w_pool, pool_scale, conv_w, w_out):
    b, s, d = x.shape
    d_pool = pool_scale.shape[-1]
    tm = MIX_TM
    mod_spec = pl.BlockSpec((1, 1, d), lambda bi, si: (bi, 0, 0))
    return pl.pallas_call(
        functools.partial(_mix_even_kernel, tm=tm, d_pool=d_pool),
        grid=(b, s // tm),
        in_specs=[
            pl.BlockSpec((1, tm, d), lambda bi, si: (bi, si, 0)),
            mod_spec, mod_spec, mod_spec,
            _resident((1, d)),
            _resident(w_in.shape),
            _resident(w_pool.shape),
            _resident((1, d_pool)),
            _resident((CONV_WIDTH, d_pool)),
            _resident(w_out.shape),
        ],
        out_specs=pl.BlockSpec((1, tm, d), lambda bi, si: (bi, si, 0)),
        out_shape=jax.ShapeDtypeStruct(x.shape, F32),
        scratch_shapes=[pltpu.VMEM((tm + MAX_WINDOW, d_pool), F32), pltpu.VMEM((tm + CONV_HALO, d_pool), F32)],
        compiler_params=_params("arbitrary", "arbitrary"),
        name="mix_even",
    )(x, shift, scale, gate, g.reshape(1, d), w_in, w_pool, pool_scale.reshape(1, d_pool),
      conv_w.reshape(CONV_WIDTH, d_pool), w_out)


def _mix_odd_kernel(x_ref, shift_ref, scale_ref, gate_ref, g_ref, win_ref, sgun_ref, wsp_ref, bsp_ref, wout_ref,
                    o_ref, *, tm, d_sgu):
    head = d_sgu // N_SGU_HEADS
    x = x_ref[0]
    h = _modulate(x, g_ref[...], shift_ref[0], scale_ref[0]).astype(BF16)
    z = jax.nn.gelu(jnp.dot(h, win_ref[...], preferred_element_type=F32))
    u = z[:, :d_sgu]
    v = _rmsnorm(z[:, d_sgu:], sgun_ref[...]).astype(BF16)

    row = lax.broadcasted_iota(jnp.int32, (CHUNK, CHUNK), 0)
    col = lax.broadcasted_iota(jnp.int32, (CHUNK, CHUNK), 1)
    causal = (row >= col).astype(F32)
    chunks = []
    for ci in range(tm // CHUNK):
        rows = slice(ci * CHUNK, (ci + 1) * CHUNK)
        heads = []
        for hi in range(N_SGU_HEADS):
            w = (wsp_ref[hi] * causal).astype(BF16)
            vh = v[rows, hi * head:(hi + 1) * head]
            heads.append(jnp.dot(w, vh, preferred_element_type=F32) + bsp_ref[hi])
        chunks.append(jnp.concatenate(heads, axis=-1))
    mixed = jnp.concatenate(chunks, axis=0)
    y = jnp.dot((u * mixed).astype(BF16), wout_ref[...], preferred_element_type=F32)
    o_ref[0] = x + gate_ref[0] * y


def _mix_odd(x, shift, scale, gate, g, w_in, sgu_norm, w_spatial, b_spatial, w_out):
    b, s, d = x.shape
    d_sgu = sgu_norm.shape[-1]
    tm = MIX_TM
    mod_spec = pl.BlockSpec((1, 1, d), lambda bi, si: (bi, 0, 0))
    return pl.pallas_call(
        functools.partial(_mix_odd_kernel, tm=tm, d_sgu=d_sgu),
        grid=(b, s // tm),
        in_specs=[
            pl.BlockSpec((1, tm, d), lambda bi, si: (bi, si, 0)),
            mod_spec, mod_spec, mod_spec,
            _resident((1, d)),
            _resident(w_in.shape),
            _resident((1, d_sgu)),
            _resident(w_spatial.shape),
            _resident((N_SGU_HEADS, CHUNK, 1)),
            _resident(w_out.shape),
        ],
        out_specs=pl.BlockSpec((1, tm, d), lambda bi, si: (bi, si, 0)),
        out_shape=jax.ShapeDtypeStruct(x.shape, F32),
        compiler_params=_params("arbitrary", "arbitrary"),
        name="mix_odd",
    )(x, shift, scale, gate, g.reshape(1, d), w_in, sgu_norm.reshape(1, d_sgu), w_spatial,
      b_spatial.reshape(N_SGU_HEADS, CHUNK, 1), w_out)


def _first_max(vals):
    best_i = jnp.zeros_like(vals[0])
    best_v = vals[0]
    for j in range(1, len(vals)):
        upd = vals[j] > best_v
        best_i = jnp.where(upd, float(j), best_i)
        best_v = jnp.where(upd, vals[j], best_v)
    return best_i, best_v


def _pick(index, vals):
    out = vals[0]
    for j in range(1, len(vals)):
        out = jnp.where(index == float(j), vals[j], out)
    return out


def _route_kernel(x_ref, shift_ref, scale_ref, g_ref, wr_ref, bias_ref, h_ref, meta_ref, cnt_ref, carry, *, tm):
    i = pl.program_id(0)

    @pl.when(i == 0)
    def _():
        carry[...] = jnp.zeros_like(carry)

    h = _modulate(x_ref[...], g_ref[...], shift_ref[0], scale_ref[0])
    h_ref[...] = h

    h_hi = h.astype(BF16)
    h_lo = (h - h_hi.astype(F32)).astype(BF16)
    wr = wr_ref[...]
    w_hi = wr.astype(BF16)
    w_lo = (wr - w_hi.astype(F32)).astype(BF16)
    both = jnp.dot(h_hi, jnp.concatenate([w_hi, w_lo], axis=-1), preferred_element_type=F32)
    logits = both[:, :LANES] + both[:, LANES:] + jnp.dot(h_lo, w_hi, preferred_element_type=F32)

    aff = jax.nn.sigmoid(logits.T[:N_EXPERTS])
    sel = aff + bias_ref[...]
    sel_rows = [sel[e:e + 1] for e in range(N_EXPERTS)]
    aff_rows = [aff[e:e + 1] for e in range(N_EXPERTS)]

    scores = []
    for gi in range(N_EXPERT_GROUPS):
        r = sel_rows[gi * EXPERTS_PER_GROUP:(gi + 1) * EXPERTS_PER_GROUP]
        pair = None
        for a in range(EXPERTS_PER_GROUP):
            for b in range(a + 1, EXPERTS_PER_GROUP):
                ps = r[a] + r[b]
                pair = ps if pair is None else jnp.maximum(pair, ps)
        scores.append(pair)
    best, _ = _first_max(scores)

    sel_in = [_pick(best, [sel_rows[gi * EXPERTS_PER_GROUP + j] for gi in range(N_EXPERT_GROUPS)])
              for j in range(EXPERTS_PER_GROUP)]
    aff_in = [_pick(best, [aff_rows[gi * EXPERTS_PER_GROUP + j] for gi in range(N_EXPERT_GROUPS)])
              for j in range(EXPERTS_PER_GROUP)]
    i1, _ = _first_max(sel_in)
    i2, _ = _first_max([jnp.where(i1 == float(j), -jnp.inf, sel_in[j]) for j in range(EXPERTS_PER_GROUP)])
    a1 = _pick(i1, aff_in)
    a2 = _pick(i2, aff_in)
    denom = a1 + a2
    e1 = best * float(EXPERTS_PER_GROUP) + i1
    e2 = best * float(EXPERTS_PER_GROUP) + i2

    eio = lax.broadcasted_iota(jnp.int32, (N_EXPERTS, tm), 0).astype(F32)
    hit1 = eio == e1
    hit2 = eio == e2
    member = jnp.logical_or(hit1, hit2).astype(F32)
    before = (lax.broadcasted_iota(jnp.int32, (tm, tm), 0) < lax.broadcasted_iota(jnp.int32, (tm, tm), 1))
    prior = jnp.dot(member.astype(BF16), before.astype(BF16), preferred_element_type=F32) + carry[:, 0:1]
    r1 = jnp.sum(jnp.where(hit1, prior, 0.0), axis=0, keepdims=True)
    r2 = jnp.sum(jnp.where(hit2, prior, 0.0), axis=0, keepdims=True)
    carry[...] = carry[...] + jnp.sum(member, axis=1, keepdims=True)
    cnt_ref[...] = carry[...]

    zero = jnp.zeros_like(e1)
    meta_ref[...] = jnp.concatenate([e1, e2, a1 / denom, a2 / denom, r1, r2, zero, zero], axis=0)


def _route(x2, shift, scale, g, w_router, router_bias, seq):
    t, d = x2.shape
    tm = ROUTE_TM
    per_seq = seq // tm
    mod_spec = pl.BlockSpec((1, 1, d), lambda i: (i // per_seq, 0, 0))
    wr_pad = jnp.zeros((d, LANES), F32).at[:, :N_EXPERTS].set(w_router)
    return pl.pallas_call(
        functools.partial(_route_kernel, tm=tm),
        grid=(t // tm,),
        in_specs=[
            pl.BlockSpec((tm, d), lambda i: (i, 0)),
            mod_spec, mod_spec,
            _resident((1, d)),
            _resident((d, LANES)),
            _resident((N_EXPERTS, 1)),
        ],
        out_specs=[
            pl.BlockSpec((tm, d), lambda i: (i, 0)),
            pl.BlockSpec((8, tm), lambda i: (0, i)),
            pl.BlockSpec((N_EXPERTS, LANES), lambda i: (0, 0)),
        ],
        out_shape=[
            jax.ShapeDtypeStruct((t, d), F32),
            jax.ShapeDtypeStruct((8, t), F32),
            jax.ShapeDtypeStruct((N_EXPERTS, LANES), F32),
        ],
        scratch_shapes=[pltpu.VMEM((N_EXPERTS, LANES), F32)],
        compiler_params=_params("arbitrary"),
        name="route",
    )(x2, shift, scale, g.reshape(1, d), wr_pad, router_bias.reshape(N_EXPERTS, 1))


def _row_gather(src_hbm, dst, sem, index_of_row, rows):
    def body(r, carry):
        pltpu.make_async_copy(src_hbm.at[pl.ds(index_of_row(r), 1)], dst.at[pl.ds(r, 1)], sem).start()
        return carry
    lax.fori_loop(0, rows, body, 0, unroll=8)


def _experts_kernel(te_ref, inv_ref, nt_ref, h_hbm, wg_ref, wu_ref, wd_ref, o_ref, buf, sem, *, tm):
    i = pl.program_id(0)
    nt = nt_ref[0]
    slot = i % 2

    def start(tile, slot_):
        _row_gather(h_hbm, buf.at[slot_], sem.at[slot_], lambda r: inv_ref[tile * tm + r], tm)

    @pl.when(i == 0)
    def _():
        start(0, 0)

    @pl.when(i + 1 < nt)
    def _():
        start(i + 1, 1 - slot)

    @pl.when(i < nt)
    def _():
        pltpu.make_async_copy(h_hbm.at[pl.ds(0, tm)], buf.at[slot], sem.at[slot]).wait()
        xs = buf[slot].astype(BF16)
        gate = jnp.dot(xs, wg_ref[0].astype(BF16), preferred_element_type=F32)
        up = jnp.dot(xs, wu_ref[0].astype(BF16), preferred_element_type=F32)
        act = (jax.nn.silu(gate) * up).astype(BF16)
        o_ref[...] = jnp.dot(act, wd_ref[0].astype(BF16), preferred_element_type=F32)

    @pl.when(i >= nt)
    def _():
        o_ref[...] = jnp.zeros_like(o_ref)


def _experts(h, tile_expert, inv, n_tiles, w_gate, w_up, w_down, max_tiles):
    t, d = h.shape
    f = w_gate.shape[-1]
    tm = EXPERT_TM
    grid_spec = pltpu.PrefetchScalarGridSpec(
        num_scalar_prefetch=3,
        grid=(max_tiles,),
        in_specs=[
            pl.BlockSpec(memory_space=pl.ANY),
            pl.BlockSpec((1, d, f), lambda i, te, inv_, nt: (te[i], 0, 0)),
            pl.BlockSpec((1, d, f), lambda i, te, inv_, nt: (te[i], 0, 0)),
            pl.BlockSpec((1, f, d), lambda i, te, inv_, nt: (te[i], 0, 0)),
        ],
        out_specs=pl.BlockSpec((tm, d), lambda i, te, inv_, nt: (i, 0)),
        scratch_shapes=[pltpu.VMEM((2, tm, d), F32), pltpu.SemaphoreType.DMA((2,))],
    )
    return pl.pallas_call(
        functools.partial(_experts_kernel, tm=tm),
        grid_spec=grid_spec,
        out_shape=jax.ShapeDtypeStruct((max_tiles * tm, d), F32),
        compiler_params=_params("arbitrary"),
        name="experts",
    )(tile_expert, inv, n_tiles, h, w_gate, w_up, w_down)


def _combine_kernel(dest_ref, x_ref, gate_ref, wts_ref, ys_hbm, fn_ref, o_ref, buf, sem, *, tm, t_total, final):
    i = pl.program_id(0)
    n = pl.num_programs(0)
    slot = i % 2

    def start(tile, slot_):
        for k in range(2):
            _row_gather(ys_hbm, buf.at[slot_, k], sem.at[slot_],
                        lambda r, k=k: dest_ref[k * t_total + tile * tm + r], tm)

    @pl.when(i == 0)
    def _():
        start(0, 0)

    @pl.when(i + 1 < n)
    def _():
        start(i + 1, 1 - slot)

    for k in range(2):
        pltpu.make_async_copy(ys_hbm.at[pl.ds(0, tm)], buf.at[slot, k], sem.at[slot]).wait()
    w = wts_ref[...]
    moe = w[:, 0:1] * buf[slot, 0] + w[:, 1:2] * buf[slot, 1]
    out = x_ref[...] + gate_ref[0] * moe
    if final:
        out = _rmsnorm(out, fn_ref[...])
    o_ref[...] = out


def _combine(x2, gate, wts, dest, ys, final_norm, seq, final):
    t, d = x2.shape
    tm = COMBINE_TM
    per_seq = seq // tm
    grid_spec = pltpu.PrefetchScalarGridSpec(
        num_scalar_prefetch=1,
        grid=(t // tm,),
        in_specs=[
            pl.BlockSpec((tm, d), lambda i, dest_: (i, 0)),
            pl.BlockSpec((1, 1, d), lambda i, dest_: (i // per_seq, 0, 0)),
            pl.BlockSpec((tm, 2), lambda i, dest_: (i, 0)),
            pl.BlockSpec(memory_space=pl.ANY),
            pl.BlockSpec((1, d), lambda i, dest_: (0, 0)),
        ],
        out_specs=pl.BlockSpec((tm, d), lambda i, dest_: (i, 0)),
        scratch_shapes=[pltpu.VMEM((2, 2, tm, d), F32), pltpu.SemaphoreType.DMA((2,))],
    )
    return pl.pallas_call(
        functools.partial(_combine_kernel, tm=tm, t_total=t, final=final),
        grid_spec=grid_spec,
        out_shape=jax.ShapeDtypeStruct((t, d), F32),
        compiler_params=_params("arbitrary"),
        name="combine_final" if final else "combine",
    )(dest, x2, gate, wts, ys, final_norm.reshape(1, d))


def _moe(x, shift, scale, gate, g, w_router, router_bias, w_gate, w_up, w_down, final_norm, final):
    b, s, d = x.shape
    t = b * s
    x2 = x.reshape(t, d)
    h, meta, counts = _route(x2, shift, scale, g, w_router, router_bias, s)

    tm = EXPERT_TM
    max_tiles = (2 * t) // tm + N_EXPERTS
    cnt = counts[:, 0].astype(jnp.int32)
    padded = ((cnt + tm - 1) // tm) * tm
    ends = jnp.cumsum(padded)
    offsets = ends - padded
    idx = meta[0:2].astype(jnp.int32)
    rank = meta[4:6].astype(jnp.int32)
    dest = (offsets[idx] + rank).reshape(-1)
    tokens = jnp.tile(jnp.arange(t, dtype=jnp.int32), 2)
    inv = jnp.zeros((max_tiles * tm,), jnp.int32).at[dest].set(tokens)
    n_tiles = (ends[-1] // tm).astype(jnp.int32)
    tile_start = jnp.arange(max_tiles, dtype=jnp.int32) * tm
    tile_expert = jnp.sum((tile_start[:, None] >= ends[None, :]).astype(jnp.int32), axis=1)
    tile_expert = jnp.where(jnp.arange(max_tiles) < n_tiles, tile_expert, tile_expert[n_tiles - 1])
    tile_expert = jnp.clip(tile_expert, 0, N_EXPERTS - 1)

    ys = _experts(h, tile_expert, inv, n_tiles.reshape(1), w_gate, w_up, w_down, max_tiles)
    out = _combine(x2, gate, meta[2:4].T, dest, ys, final_norm, s, final)
    return out.reshape(b, s, d)


def kernel(x, c, ada_w, ada_b, norm_mix, norm_ffn, w_in_even, w_pool, pool_scale, conv_w, w_out_even, w_in_odd,
           sgu_norm, w_spatial, b_spatial, w_out_odd, w_router, router_bias, w_gate, w_up, w_down, final_norm):
    depth = ada_w.shape[0]
    b, s, d = x.shape
    mod = _ada(c, ada_w, ada_b)
    for l in range(depth):
        shift_m, scale_m, gate_m, shift_f, scale_f, gate_f = [
            m.reshape(b, 1, d) for m in jnp.split(mod[l], N_MOD, axis=-1)]
        i = l // 2
        if l % 2 == 0:
            x = _mix_even(x, shift_m, scale_m, gate_m, norm_mix[l], w_in_even[i].astype(BF16),
                          w_pool[i].astype(BF16), pool_scale[i], conv_w[i], w_out_even[i].astype(BF16))
        else:
            x = _mix_odd(x, shift_m, scale_m, gate_m, norm_mix[l], w_in_odd[i].astype(BF16), sgu_norm[i],
                         w_spatial[i], b_spatial[i], w_out_odd[i].astype(BF16))
        x = _moe(x, shift_f, scale_f, gate_f, norm_ffn[l], w_router, router_bias, w_gate[l], w_up[l], w_down[l],
                 final_norm, final=(l == depth - 1))
    return x
```

```python
import functools

import jax
import jax.numpy as jnp
from jax import lax
from jax.experimental import pallas as pl
from jax.experimental.pallas import tpu as pltpu

F32 = jnp.float32
BF16 = jnp.bfloat16

EPS = 1e-6
POOL_WINDOWS = (2, 4, 8, 16)
MAX_WINDOW = max(POOL_WINDOWS)
CONV_WIDTH = 3
CONV_HALO = 8
CHUNK = 128
N_SGU_HEADS = 8
N_EXPERTS = 16
N_EXPERT_GROUPS = 4
EXPERTS_PER_GROUP = N_EXPERTS // N_EXPERT_GROUPS
N_MOD = 6
LANES = 128

ADA_ROWS = 16
ADA_TN = 1024
MIX_TM = 256
ROUTE_TM = 512
EXPERT_TM = 256
COMBINE_TM = 256
VMEM_LIMIT = 56 * 1024 * 1024


def _rmsnorm(x, g):
    return x * lax.rsqrt(jnp.mean(x * x, axis=-1, keepdims=True) + EPS) * g


def _modulate(x, g, shift, scale):
    return _rmsnorm(x, g) * (1.0 + scale) + shift


def _params(*sem):
    return pltpu.CompilerParams(dimension_semantics=sem, vmem_limit_bytes=VMEM_LIMIT)


def _resident(shape):
    return pl.BlockSpec(shape, lambda *_: (0,) * len(shape), pipeline_mode=pl.Buffered(1))


def _ada_kernel(c_ref, w_ref, b_ref, o_ref):
    c_act = jax.nn.silu(c_ref[...]).astype(BF16)
    o_ref[0] = jnp.dot(c_act, w_ref[0].astype(BF16), preferred_element_type=F32) + b_ref[0]


def _ada(c, ada_w, ada_b):
    depth, d, n = ada_w.shape
    batch = c.shape[0]
    c_pad = jnp.zeros((ADA_ROWS, d), F32).at[:batch].set(c)
    out = pl.pallas_call(
        _ada_kernel,
        grid=(depth, n // ADA_TN),
        in_specs=[
            pl.BlockSpec((ADA_ROWS, d), lambda l, j: (0, 0)),
            pl.BlockSpec((1, d, ADA_TN), lambda l, j: (l, 0, j)),
            pl.BlockSpec((1, 1, ADA_TN), lambda l, j: (l, 0, j)),
        ],
        out_specs=pl.BlockSpec((1, ADA_ROWS, ADA_TN), lambda l, j: (l, 0, j)),
        out_shape=jax.ShapeDtypeStruct((depth, ADA_ROWS, n), F32),
        compiler_params=_params("arbitrary", "arbitrary"),
        name="ada",
    )(c_pad, ada_w, ada_b.reshape(depth, 1, n))
    return out[:, :batch]


def _mix_even_kernel(x_ref, shift_ref, scale_ref, gate_ref, g_ref, win_ref, wpool_ref, pscale_ref, cw_ref,
                     wout_ref, o_ref, ext_pool, ext_conv, *, tm, d_pool):
    s = pl.program_id(1)
    group = d_pool // len(POOL_WINDOWS)

    @pl.when(s == 0)
    def _():
        ext_pool[0:MAX_WINDOW, :] = jnp.zeros((MAX_WINDOW, d_pool), F32)
        ext_conv[0:CONV_HALO, :] = jnp.zeros((CONV_HALO, d_pool), F32)

    x = x_ref[0]
    h = _modulate(x, g_ref[...], shift_ref[0], scale_ref[0]).astype(BF16)
    z = jnp.dot(h, win_ref[...], preferred_element_type=F32)
    u = z[:, :d_pool]
    z_h = z[:, d_pool:2 * d_pool]
    z_b = z[:, 2 * d_pool:3 * d_pool]
    z_c = z[:, 3 * d_pool:]
    ext_pool[MAX_WINDOW:MAX_WINDOW + tm, :] = u
    ext_conv[CONV_HALO:CONV_HALO + tm, :] = z_c * z_h

    pos = (s * tm + 1 + lax.broadcasted_iota(jnp.int32, (tm, 1), 0)).astype(F32)
    parts = []
    for gi, w in enumerate(POOL_WINDOWS):
        cols = slice(gi * group, (gi + 1) * group)
        acc = ext_pool[MAX_WINDOW:MAX_WINDOW + tm, cols]
        for j in range(1, w):
            acc = acc + ext_pool[MAX_WINDOW - j:MAX_WINDOW - j + tm, cols]
        pooled = (acc / jnp.minimum(pos, float(w)) - u[:, cols]).astype(BF16)
        parts.append(jnp.dot(pooled, wpool_ref[gi], preferred_element_type=F32) * pscale_ref[:, cols])

    conv = cw_ref[0:1, :] * ext_conv[CONV_HALO - 2:CONV_HALO - 2 + tm, :]
    for k in range(1, CONV_WIDTH):
        lo = CONV_HALO - (CONV_WIDTH - 1) + k
        conv = conv + cw_ref[k:k + 1, :] * ext_conv[lo:lo + tm, :]
    parts.append(z_b * conv)

    mixed = jnp.concatenate(parts, axis=-1).astype(BF16)
    y = jnp.dot(mixed, wout_ref[...], preferred_element_type=F32)
    o_ref[0] = x + gate_ref[0] * y

    ext_pool[0:MAX_WINDOW, :] = ext_pool[tm:tm + MAX_WINDOW, :]
    ext_conv[0:CONV_HALO, :] = ext_conv[tm:tm + CONV_HALO, :]


def _mix_even(x, shift, scale, gate, g, w_in, w_pool, pool_scale, conv_w, w_out):
    b, s, d = x.shape
    d_pool = pool_scale.shape[-1]
    tm = MIX_TM
    mod_spec = pl.BlockSpec((1, 1, d), lambda bi, si: (bi, 0, 0))
    return pl.pallas_call(
        functools.partial(_mix_even_kernel, tm=tm, d_pool=d_pool),
        grid=(b, s // tm),
        in_specs=[
            pl.BlockSpec((1, tm, d), lambda bi, si: (bi, si, 0)),
            mod_spec, mod_spec, mod_spec,
            _resident((1, d)),
            _resident(w_in.shape),
            _resident(w_pool.shape),
            _resident((1, d_pool)),
            _resident((CONV_WIDTH, d_pool)),
            _resident(w_out.shape),
        ],
        out_specs=pl.BlockSpec((1, tm, d), lambda bi, si: (bi, si, 0)),
        out_shape=jax.ShapeDtypeStruct(x.shape, F32),
        scratch_shapes=[pltpu.VMEM((tm + MAX_WINDOW, d_pool), F32), pltpu.VMEM((tm + CONV_HALO, d_pool), F32)],
        compiler_params=_params("arbitrary", "arbitrary"),
        name="mix_even",
    )(x, shift, scale, gate, g.reshape(1, d), w_in, w_pool, pool_scale.reshape(1, d_pool),
      conv_w.reshape(CONV_WIDTH, d_pool), w_out)


def _mix_odd_kernel(x_ref, shift_ref, scale_ref, gate_ref, g_ref, win_ref, sgun_ref, wsp_ref, bsp_ref, wout_ref,
                    o_ref, *, tm, d_sgu):
    head = d_sgu // N_SGU_HEADS
    x = x_ref[0]
    h = _modulate(x, g_ref[...], shift_ref[0], scale_ref[0]).astype(BF16)
    z = jax.nn.gelu(jnp.dot(h, win_ref[...], preferred_element_type=F32))
    u = z[:, :d_sgu]
    v = _rmsnorm(z[:, d_sgu:], sgun_ref[...]).astype(BF16)

    row = lax.broadcasted_iota(jnp.int32, (CHUNK, CHUNK), 0)
    col = lax.broadcasted_iota(jnp.int32, (CHUNK, CHUNK), 1)
    causal = (row >= col).astype(F32)
    chunks = []
    for ci in range(tm // CHUNK):
        rows = slice(ci * CHUNK, (ci + 1) * CHUNK)
        heads = []
        for hi in range(N_SGU_HEADS):
            w = (wsp_ref[hi] * causal).astype(BF16)
            vh = v[rows, hi * head:(hi + 1) * head]
            heads.append(jnp.dot(w, vh, preferred_element_type=F32) + bsp_ref[hi])
        chunks.append(jnp.concatenate(heads, axis=-1))
    mixed = jnp.concatenate(chunks, axis=0)
    y = jnp.dot((u * mixed).astype(BF16), wout_ref[...], preferred_element_type=F32)
    o_ref[0] = x + gate_ref[0] * y


def _mix_odd(x, shift, scale, gate, g, w_in, sgu_norm, w_spatial, b_spatial, w_out):
    b, s, d = x.shape
    d_sgu = sgu_norm.shape[-1]
    tm = MIX_TM
    mod_spec = pl.BlockSpec((1, 1, d), lambda bi, si: (bi, 0, 0))
    return pl.pallas_call(
        functools.partial(_mix_odd_kernel, tm=tm, d_sgu=d_sgu),
        grid=(b, s // tm),
        in_specs=[
            pl.BlockSpec((1, tm, d), lambda bi, si: (bi, si, 0)),
            mod_spec, mod_spec, mod_spec,
            _resident((1, d)),
            _resident(w_in.shape),
            _resident((1, d_sgu)),
            _resident(w_spatial.shape),
            _resident((N_SGU_HEADS, CHUNK, 1)),
            _resident(w_out.shape),
        ],
        out_specs=pl.BlockSpec((1, tm, d), lambda bi, si: (bi, si, 0)),
        out_shape=jax.ShapeDtypeStruct(x.shape, F32),
        compiler_params=_params("arbitrary", "arbitrary"),
        name="mix_odd",
    )(x, shift, scale, gate, g.reshape(1, d), w_in, sgu_norm.reshape(1, d_sgu), w_spatial,
      b_spatial.reshape(N_SGU_HEADS, CHUNK, 1), w_out)


def _first_max(vals):
    best_i = jnp.zeros_like(vals[0])
    best_v = vals[0]
    for j in range(1, len(vals)):
        upd = vals[j] > best_v
        best_i = jnp.where(upd, float(j), best_i)
        best_v = jnp.where(upd, vals[j], best_v)
    return best_i, best_v


def _pick(index, vals):
    out = vals[0]
    for j in range(1, len(vals)):
        out = jnp.where(index == float(j), vals[j], out)
    return out


def _route_kernel(x_ref, shift_ref, scale_ref, g_ref, wr_ref, bias_ref, h_ref, meta_ref, cnt_ref, carry, *, tm):
    i = pl.program_id(0)

    @pl.when(i == 0)
    def _():
        carry[...] = jnp.zeros_like(carry)

    h = _modulate(x_ref[...], g_ref[...], shift_ref[0], scale_ref[0])
    h_ref[...] = h

    h_hi = h.astype(BF16)
    h_lo = (h - h_hi.astype(F32)).astype(BF16)
    wr = wr_ref[...]
    w_hi = wr.astype(BF16)
    w_lo = (wr - w_hi.astype(F32)).astype(BF16)
    both = jnp.dot(h_hi, jnp.concatenate([w_hi, w_lo], axis=-1), preferred_element_type=F32)
    logits = both[:, :LANES] + both[:, LANES:] + jnp.dot(h_lo, w_hi, preferred_element_type=F32)

    aff = jax.nn.sigmoid(logits.T[:N_EXPERTS])
    sel = aff + bias_ref[...]
    sel_rows = [sel[e:e + 1] for e in range(N_EXPERTS)]
    aff_rows = [aff[e:e + 1] for e in range(N_EXPERTS)]

    scores = []
    for gi in range(N_EXPERT_GROUPS):
        r = sel_rows[gi * EXPERTS_PER_GROUP:(gi + 1) * EXPERTS_PER_GROUP]
        pair = None
        for a in range(EXPERTS_PER_GROUP):
            for b in range(a + 1, EXPERTS_PER_GROUP):
                ps = r[a] + r[b]
                pair = ps if pair is None else jnp.maximum(pair, ps)
        scores.append(pair)
    best, _ = _first_max(scores)

    sel_in = [_pick(best, [sel_rows[gi * EXPERTS_PER_GROUP + j] for gi in range(N_EXPERT_GROUPS)])
              for j in range(EXPERTS_PER_GROUP)]
    aff_in = [_pick(best, [aff_rows[gi * EXPERTS_PER_GROUP + j] for gi in range(N_EXPERT_GROUPS)])
              for j in range(EXPERTS_PER_GROUP)]
    i1, _ = _first_max(sel_in)
    i2, _ = _first_max([jnp.where(i1 == float(j), -jnp.inf, sel_in[j]) for j in range(EXPERTS_PER_GROUP)])
    a1 = _pick(i1, aff_in)
    a2 = _pick(i2, aff_in)
    denom = a1 + a2
    e1 = best * float(EXPERTS_PER_GROUP) + i1
    e2 = best * float(EXPERTS_PER_GROUP) + i2

    eio = lax.broadcasted_iota(jnp.int32, (N_EXPERTS, tm), 0).astype(F32)
    hit1 = eio == e1
    hit2 = eio == e2
    member = jnp.logical_or(hit1, hit2).astype(F32)
    before = (lax.broadcasted_iota(jnp.int32, (tm, tm), 0) < lax.broadcasted_iota(jnp.int32, (tm, tm), 1))
    prior = jnp.dot(member.astype(BF16), before.astype(BF16), preferred_element_type=F32) + carry[:, 0:1]
    r1 = jnp.sum(jnp.where(hit1, prior, 0.0), axis=0, keepdims=True)
    r2 = jnp.sum(jnp.where(hit2, prior, 0.0), axis=0, keepdims=True)
    carry[...] = carry[...] + jnp.sum(member, axis=1, keepdims=True)
    cnt_ref[...] = carry[...]

    zero = jnp.zeros_like(e1)
    meta_ref[...] = jnp.concatenate([e1, e2, a1 / denom, a2 / denom, r1, r2, zero, zero], axis=0)


def _route(x2, shift, scale, g, w_router, router_bias, seq):
    t, d = x2.shape
    tm = ROUTE_TM
    per_seq = seq // tm
    mod_spec = pl.BlockSpec((1, 1, d), lambda i: (i // per_seq, 0, 0))
    wr_pad = jnp.zeros((d, LANES), F32).at[:, :N_EXPERTS].set(w_router)
    return pl.pallas_call(
        functools.partial(_route_kernel, tm=tm),
        grid=(t // tm,),
        in_specs=[
            pl.BlockSpec((tm, d), lambda i: (i, 0)),
            mod_spec, mod_spec,
            _resident((1, d)),
            _resident((d, LANES)),
            _resident((N_EXPERTS, 1)),
        ],
        out_specs=[
            pl.BlockSpec((tm, d), lambda i: (i, 0)),
            pl.BlockSpec((8, tm), lambda i: (0, i)),
            pl.BlockSpec((N_EXPERTS, LANES), lambda i: (0, 0)),
        ],
        out_shape=[
            jax.ShapeDtypeStruct((t, d), F32),
            jax.ShapeDtypeStruct((8, t), F32),
            jax.ShapeDtypeStruct((N_EXPERTS, LANES), F32),
        ],
        scratch_shapes=[pltpu.VMEM((N_EXPERTS, LANES), F32)],
        compiler_params=_params("arbitrary"),
        name="route",
    )(x2, shift, scale, g.reshape(1, d), wr_pad, router_bias.reshape(N_EXPERTS, 1))


def _row_gather(src_hbm, dst, sem, index_of_row, rows):
    def body(r, carry):
        pltpu.make_async_copy(src_hbm.at[pl.ds(index_of_row(r), 1)], dst.at[pl.ds(r, 1)], sem).start()
        return carry
    lax.fori_loop(0, rows, body, 0, unroll=8)


def _experts_kernel(te_ref, inv_ref, nt_ref, h_hbm, wg_ref, wu_ref, wd_ref, o_ref, buf, sem, *, tm):
    i = pl.program_id(0)
    nt = nt_ref[0]
    slot = i % 2

    def start(tile, slot_):
        _row_gather(h_hbm, buf.at[slot_], sem.at[slot_], lambda r: inv_ref[tile * tm + r], tm)

    @pl.when(i == 0)
    def _():
        start(0, 0)

    @pl.when(i + 1 < nt)
    def _():
        start(i + 1, 1 - slot)

    @pl.when(i < nt)
    def _():
        pltpu.make_async_copy(h_hbm.at[pl.ds(0, tm)], buf.at[slot], sem.at[slot]).wait()
        xs = buf[slot].astype(BF16)
        gate = jnp.dot(xs, wg_ref[0, 0].astype(BF16), preferred_element_type=F32)
        up = jnp.dot(xs, wu_ref[0, 0].astype(BF16), preferred_element_type=F32)
        act = (jax.nn.silu(gate) * up).astype(BF16)
        o_ref[...] = jnp.dot(act, wd_ref[0, 0].astype(BF16), preferred_element_type=F32)

    @pl.when(i >= nt)
    def _():
        o_ref[...] = jnp.zeros_like(o_ref)


def _experts(h, tile_expert, inv, n_tiles, w_gate, w_up, w_down, layer, max_tiles):
    t, d = h.shape
    f = w_gate.shape[-1]
    tm = EXPERT_TM
    grid_spec = pltpu.PrefetchScalarGridSpec(
        num_scalar_prefetch=3,
        grid=(max_tiles,),
        in_specs=[
            pl.BlockSpec(memory_space=pl.ANY),
            pl.BlockSpec((1, 1, d, f), lambda i, te, inv_, nt: (layer, te[i], 0, 0)),
            pl.BlockSpec((1, 1, d, f), lambda i, te, inv_, nt: (layer, te[i], 0, 0)),
            pl.BlockSpec((1, 1, f, d), lambda i, te, inv_, nt: (layer, te[i], 0, 0)),
        ],
        out_specs=pl.BlockSpec((tm, d), lambda i, te, inv_, nt: (i, 0)),
        scratch_shapes=[pltpu.VMEM((2, tm, d), F32), pltpu.SemaphoreType.DMA((2,))],
    )
    return pl.pallas_call(
        functools.partial(_experts_kernel, tm=tm),
        grid_spec=grid_spec,
        out_shape=jax.ShapeDtypeStruct((max_tiles * tm, d), F32),
        compiler_params=_params("arbitrary"),
        name="experts",
    )(tile_expert, inv, n_tiles, h, w_gate, w_up, w_down)


def _combine_kernel(dest_ref, x_ref, gate_ref, wts_ref, ys_hbm, fn_ref, o_ref, buf, sem, *, tm, t_total, final):
    i = pl.program_id(0)
    n = pl.num_programs(0)
    slot = i % 2

    def start(tile, slot_):
        for k in range(2):
            _row_gather(ys_hbm, buf.at[slot_, k], sem.at[slot_],
                        lambda r, k=k: dest_ref[k * t_total + tile * tm + r], tm)

    @pl.when(i == 0)
    def _():
        start(0, 0)

    @pl.when(i + 1 < n)
    def _():
        start(i + 1, 1 - slot)

    for k in range(2):
        pltpu.make_async_copy(ys_hbm.at[pl.ds(0, tm)], buf.at[slot, k], sem.at[slot]).wait()
    w = wts_ref[...]
    moe = w[:, 0:1] * buf[slot, 0] + w[:, 1:2] * buf[slot, 1]
    out = x_ref[...] + gate_ref[0] * moe
    if final:
        out = _rmsnorm(out, fn_ref[...])
    o_ref[...] = out


def _combine(x2, gate, wts, dest, ys, final_norm, seq, final):
    t, d = x2.shape
    tm = COMBINE_TM
    per_seq = seq // tm
    grid_spec = pltpu.PrefetchScalarGridSpec(
        num_scalar_prefetch=1,
        grid=(t // tm,),
        in_specs=[
            pl.BlockSpec((tm, d), lambda i, dest_: (i, 0)),
            pl.BlockSpec((1, 1, d), lambda i, dest_: (i // per_seq, 0, 0)),
            pl.BlockSpec((tm, 2), lambda i, dest_: (i, 0)),
            pl.BlockSpec(memory_space=pl.ANY),
            pl.BlockSpec((1, d), lambda i, dest_: (0, 0)),
        ],
        out_specs=pl.BlockSpec((tm, d), lambda i, dest_: (i, 0)),
        scratch_shapes=[pltpu.VMEM((2, 2, tm, d), F32), pltpu.SemaphoreType.DMA((2,))],
    )
    return pl.pallas_call(
        functools.partial(_combine_kernel, tm=tm, t_total=t, final=final),
        grid_spec=grid_spec,
        out_shape=jax.ShapeDtypeStruct((t, d), F32),
        compiler_params=_params("arbitrary"),
        name="combine_final" if final else "combine",
    )(dest, x2, gate, wts, ys, final_norm.reshape(1, d))


def _moe(x, shift, scale, gate, g, w_router, router_bias, w_gate, w_up, w_down, layer, final_norm, final):
    b, s, d = x.shape
    t = b * s
    x2 = x.reshape(t, d)
    h, meta, counts = _route(x2, shift, scale, g, w_router, router_bias, s)

    tm = EXPERT_TM
    max_tiles = (2 * t) // tm + N_EXPERTS
    cnt = counts[:, 0].astype(jnp.int32)
    padded = ((cnt + tm - 1) // tm) * tm
    ends = jnp.cumsum(padded)
    offsets = ends - padded
    idx = meta[0:2].astype(jnp.int32)
    rank = meta[4:6].astype(jnp.int32)
    hit = idx[..., None] == jnp.arange(N_EXPERTS, dtype=jnp.int32)
    dest = (jnp.sum(jnp.where(hit, offsets, 0), axis=-1) + rank).reshape(-1)
    tokens = jnp.tile(jnp.arange(t, dtype=jnp.int32), 2)
    inv = jnp.zeros((max_tiles * tm,), jnp.int32).at[dest].set(tokens)
    n_tiles = (ends[-1] // tm).astype(jnp.int32)
    tile_start = jnp.arange(max_tiles, dtype=jnp.int32) * tm
    tile_expert = jnp.sum((tile_start[:, None] >= ends[None, :]).astype(jnp.int32), axis=1)
    tile_expert = jnp.where(jnp.arange(max_tiles) < n_tiles, tile_expert, tile_expert[n_tiles - 1])
    tile_expert = jnp.clip(tile_expert, 0, N_EXPERTS - 1)

    ys = _experts(h, tile_expert, inv, n_tiles.reshape(1), w_gate, w_up, w_down, layer, max_tiles)
    out = _combine(x2, gate, meta[2:4].T, dest, ys, final_norm, s, final)
    return out.reshape(b, s, d)


def kernel(x, c, ada_w, ada_b, norm_mix, norm_ffn, w_in_even, w_pool, pool_scale, conv_w, w_out_even, w_in_odd,
           sgu_norm, w_spatial, b_spatial, w_out_odd, w_router, router_bias, w_gate, w_up, w_down, final_norm):
    depth = ada_w.shape[0]
    b, s, d = x.shape
    mod = _ada(c, ada_w, ada_b)
    for l in range(depth):
        shift_m, scale_m, gate_m, shift_f, scale_f, gate_f = [
            m.reshape(b, 1, d) for m in jnp.split(mod[l], N_MOD, axis=-1)]
        i = l // 2
        if l % 2 == 0:
            x = _mix_even(x, shift_m, scale_m, gate_m, norm_mix[l], w_in_even[i].astype(BF16),
                          w_pool[i].astype(BF16), pool_scale[i], conv_w[i], w_out_even[i].astype(BF16))
        else:
            x = _mix_odd(x, shift_m, scale_m, gate_m, norm_mix[l], w_in_odd[i].astype(BF16), sgu_norm[i],
                         w_spatial[i], b_spatial[i], w_out_odd[i].astype(BF16))
        x = _moe(x, shift_f, scale_f, gate_f, norm_ffn[l], w_router, router_bias, w_gate, w_up, w_down, l,
                 final_norm, final=(l == depth - 1))
    return x
```
